```python
import jax, jax.numpy as jnp
from jax import lax
import numpy as np

D_MODEL = 2048
BATCH = 8
SEQ = 2048
DEPTH = 1

GRID_W = 64
CTX_LEN = 256
HEAD_DIM = 128
N_Q_HEADS = 16
N_KV_HEADS = 4
Q_GROUP = N_Q_HEADS // N_KV_HEADS
ATTN_WIDTH = N_Q_HEADS * HEAD_DIM
KV_WIDTH = N_KV_HEADS * HEAD_DIM
D_RNN = D_MODEL
N_RNN_BLOCKS = 16
RNN_BLOCK = D_RNN // N_RNN_BLOCKS
CONV_WIDTH = 4
CONV_PAD_LO = 1
CONV_PAD_HI = 2
LRU_C = 8.0
D_FF = 4 * D_MODEL
N_BRANCH = 2
Q_BLOCK = 128
ROPE_THETA = 10000.0
NORM_EPS = 1e-6
N_MOD = 6
N_IN = ATTN_WIDTH + 2 * KV_WIDTH + 2 * D_RNN + N_BRANCH * D_MODEL
IN_SPLITS = (ATTN_WIDTH,
             ATTN_WIDTH + KV_WIDTH,
             ATTN_WIDTH + 2 * KV_WIDTH,
             ATTN_WIDTH + 2 * KV_WIDTH + D_RNN,
             ATTN_WIDTH + 2 * KV_WIDTH + 2 * D_RNN)

kernel_name = "hybrid_gqa_rglru_parallel_dit_block"


def rms_norm(x, g):
    xf = x.astype(jnp.float32)
    y = xf * lax.rsqrt(jnp.mean(xf * xf, axis=-1, keepdims=True) + NORM_EPS)
    return (y * g.astype(jnp.float32)).astype(x.dtype)


def modulate(h, shift, scale):
    return h * (1 + scale) + shift


def rope_tables(row_idx, col_idx):
    n_freq = HEAD_DIM // 4
    inv_freq = ROPE_THETA ** (-jnp.arange(n_freq, dtype=jnp.float32) / n_freq)
    ang = jnp.concatenate([row_idx.astype(jnp.float32)[:, None] * inv_freq,
                           col_idx.astype(jnp.float32)[:, None] * inv_freq], axis=-1)
    return jnp.cos(ang), jnp.sin(ang)


def apply_rope(x, cos, sin):
    xf = x.astype(jnp.float32).reshape(*x.shape[:-1], HEAD_DIM // 2, 2)
    x1, x2 = xf[..., 0], xf[..., 1]
    cs, sn = cos[None, :, None, :], sin[None, :, None, :]
    out = jnp.stack([x1 * cs - x2 * sn, x1 * sn + x2 * cs], axis=-1).reshape(x.shape)
    return out.astype(x.dtype)


def gqa_softmax(q_blk, k, v):
    s = jnp.einsum('bqhgd,bkhd->bhgqk', q_blk, k).astype(jnp.float32) * (HEAD_DIM ** -0.5)
    p = jax.nn.softmax(s, axis=-1)
    return jnp.einsum('bhgqk,bkhd->bqhgd', p.astype(v.dtype), v)


def latent_attention(q, k_lat, v_lat, k_ctx, v_ctx):
    b, s = q.shape[:2]
    k_all = jnp.concatenate([k_ctx, k_lat], axis=1)
    v_all = jnp.concatenate([v_ctx, v_lat], axis=1)
    nb = s // Q_BLOCK
    qb = q.reshape(b, nb, Q_BLOCK, N_KV_HEADS, Q_GROUP, HEAD_DIM).transpose(1, 0, 2, 3, 4, 5)
    o = lax.map(lambda q_blk: gqa_softmax(q_blk, k_all, v_all), qb)
    return o.transpose(1, 0, 2, 3, 4, 5).reshape(b, s, ATTN_WIDTH)


def context_attention(q_c, k_c, v_c):
    b, l = q_c.shape[:2]
    o = gqa_softmax(q_c.reshape(b, l, N_KV_HEADS, Q_GROUP, HEAD_DIM), k_c, v_c)
    return o.reshape(b, l, ATTN_WIDTH)


def depthwise_conv(x, w, bias):
    s = x.shape[1]
    xp = jnp.pad(x, ((0, 0), (CONV_PAD_LO, CONV_PAD_HI), (0, 0)))
    y = bias
    for k in range(CONV_WIDTH):
        y = y + xp[:, k:k + s] * w[k]
    return y


def rglru_coeffs(x, w_r, b_r, w_i, b_i, lam):
    b, s, _ = x.shape
    xb = x.reshape(b, s, N_RNN_BLOCKS, RNN_BLOCK)
    r = jax.nn.sigmoid(jnp.einsum('bshi,hij->bshj', xb, w_r).reshape(b, s, D_RNN) + b_r)
    i = jax.nn.sigmoid(jnp.einsum('bshi,hij->bshj', xb, w_i).reshape(b, s, D_RNN) + b_i)
    log_a = -LRU_C * r.astype(jnp.float32) * jax.nn.softplus(-lam.astype(jnp.float32))
    a = jnp.exp(log_a)
    mult = jnp.sqrt(-jnp.expm1(2.0 * log_a))
    return a, mult * (i * x).astype(jnp.float32)


def linear_scan(a, bx, reverse):
    def combine(e1, e2):
        a1, b1 = e1
        a2, b2 = e2
        return a1 * a2, a2 * b1 + b2
    _, h = lax.associative_scan(combine, (a, bx), axis=1, reverse=reverse)
    return h


def rglru_direction(x_ctx, x_lat, w_r, b_r, w_i, b_i, lam, reverse):
    a_c, bx_c = rglru_coeffs(x_ctx, w_r, b_r, w_i, b_i, lam)
    h_c = linear_scan(a_c, bx_c, reverse)
    idx = 0 if reverse else -1
    h0 = h_c[:, idx]
    a_l, bx_l = rglru_coeffs(x_lat, w_r, b_r, w_i, b_i, lam)
    first = -1 if reverse else 0
    bx_l = bx_l.at[:, first].add(a_l[:, first] * h0)
    h_l = linear_scan(a_l, bx_l, reverse)
    return h_c, h_l


def merge_branches(attn_o, rnn_h, xg, gl, w_o_attn, w_o_rnn, w_out):
    y_attn = attn_o @ w_o_attn
    y_rnn = (rnn_h * jax.nn.gelu(xg)) @ w_o_rnn
    g_attn, g_rnn = jnp.split(jax.nn.sigmoid(gl), N_BRANCH, axis=-1)
    return (g_attn * y_attn + g_rnn * y_rnn) @ w_out


def mixer(h_lat, h_ctx, cos, sin, w_in, q_gain, k_gain, conv_w, conv_b, w_rg, b_rg, w_ig, b_ig,
          lam, w_o_attn, w_o_rnn, w_out, update_ctx):
    b, s, _ = h_lat.shape
    l = h_ctx.shape[1]
    q, k, v, xr, xg, gl = jnp.split(h_lat @ w_in, IN_SPLITS, axis=-1)
    qc, kc, vc, xrc, xgc, glc = jnp.split(h_ctx @ w_in, IN_SPLITS, axis=-1)

    q = apply_rope(rms_norm(q.reshape(b, s, N_Q_HEADS, HEAD_DIM), q_gain), cos, sin)
    k = apply_rope(rms_norm(k.reshape(b, s, N_KV_HEADS, HEAD_DIM), k_gain), cos, sin)
    v = v.reshape(b, s, N_KV_HEADS, HEAD_DIM)
    kc = rms_norm(kc.reshape(b, l, N_KV_HEADS, HEAD_DIM), k_gain)
    vc = vc.reshape(b, l, N_KV_HEADS, HEAD_DIM)
    attn_lat = latent_attention(q, k, v, kc, vc)

    xr_lat = depthwise_conv(xr, conv_w, conv_b)
    xr_ctx = depthwise_conv(xrc, conv_w, conv_b)
    hc_f, hl_f = rglru_direction(xr_ctx, xr_lat, w_rg[0], b_rg[0], w_ig[0], b_ig[0], lam[0], False)
    hc_b, hl_b = rglru_direction(xr_ctx, xr_lat, w_rg[1], b_rg[1], w_ig[1], b_ig[1], lam[1], True)
    rnn_lat = (hl_f + hl_b).astype(h_lat.dtype)

    out_lat = merge_branches(attn_lat, rnn_lat, xg, gl, w_o_attn, w_o_rnn, w_out)
    out_ctx = None
    if update_ctx:
        qc = rms_norm(qc.reshape(b, l, N_Q_HEADS, HEAD_DIM), q_gain)
        attn_ctx = context_attention(qc, kc, vc)
        rnn_ctx = (hc_f + hc_b).astype(h_ctx.dtype)
        out_ctx = merge_branches(attn_ctx, rnn_ctx, xgc, glc, w_o_attn, w_o_rnn, w_out)
    return out_lat, out_ctx


def sq_relu_mlp(h, w_up, w_down):
    return jnp.square(jax.nn.relu(h @ w_up)) @ w_down


def setup_inputs(seed: int = 0) -> dict:
    key = jax.random.key(seed)
    ks = jax.random.split(key, 24)
    f32 = jnp.float32
    nrm = lambda k, shape, scale: jax.random.normal(k, shape, f32) * scale
    a0 = jax.random.uniform(ks[17], (DEPTH, 2, D_RNN), f32, 0.9, 0.999)
    sig = a0 ** (1.0 / LRU_C)
    lru_lambda = jnp.log(sig) - jnp.log1p(-sig)
    return {
        "x": nrm(ks[0], (BATCH, SEQ, D_MODEL), 1.0),
        "c": nrm(ks[1], (BATCH, D_MODEL), 1.0),
        "ctx": nrm(ks[2], (BATCH, CTX_LEN, D_MODEL), 1.0),
        "c_ctx": nrm(ks[3], (D_MODEL,), 1.0),
        "w_mod": nrm(ks[4], (DEPTH, D_MODEL, N_MOD * D_MODEL), 0.5 * D_MODEL ** -0.5),
        "b_mod": nrm(ks[5], (DEPTH, N_MOD * D_MODEL), 0.01),
        "g_mix": 1.0 + nrm(ks[6], (DEPTH, D_MODEL), 0.1),
        "g_mlp": 1.0 + nrm(ks[7], (DEPTH, D_MODEL), 0.1),
        "w_in": nrm(ks[8], (DEPTH, D_MODEL, N_IN), D_MODEL ** -0.5),
        "q_gain": 1.0 + nrm(ks[9], (DEPTH, HEAD_DIM), 0.1),
        "k_gain": 1.0 + nrm(ks[10], (DEPTH, HEAD_DIM), 0.1),
        "conv_w": nrm(ks[11], (DEPTH, CONV_WIDTH, D_RNN), CONV_WIDTH ** -0.5),
        "conv_b": nrm(ks[12], (DEPTH, D_RNN), 0.01),
        "w_rg": nrm(ks[13], (DEPTH, 2, N_RNN_BLOCKS, RNN_BLOCK, RNN_BLOCK), RNN_BLOCK ** -0.5),
        "b_rg": nrm(ks[14], (DEPTH, 2, D_RNN), 0.01),
        "w_ig": nrm(ks[15], (DEPTH, 2, N_RNN_BLOCKS, RNN_BLOCK, RNN_BLOCK), RNN_BLOCK ** -0.5),
        "b_ig": nrm(ks[16], (DEPTH, 2, D_RNN), 0.01),
        "lru_lambda": lru_lambda,
        "w_o_attn": nrm(ks[18], (DEPTH, ATTN_WIDTH, D_MODEL), ATTN_WIDTH ** -0.5),
        "w_o_rnn": nrm(ks[19], (DEPTH, D_RNN, D_MODEL), D_RNN ** -0.5),
        "w_out": nrm(ks[20], (DEPTH, D_MODEL, D_MODEL), D_MODEL ** -0.5),
        "w_up": nrm(ks[21], (DEPTH, D_MODEL, D_FF), D_MODEL ** -0.5),
        "w_down": nrm(ks[22], (DEPTH, D_FF, D_MODEL), D_FF ** -0.5),
        "g_final": 1.0 + nrm(ks[23], (D_MODEL,), 0.1),
    }


def reference(x, c, ctx, c_ctx, w_mod, b_mod, g_mix, g_mlp, w_in, q_gain, k_gain, conv_w, conv_b,
              w_rg, b_rg, w_ig, b_ig, lru_lambda, w_o_attn, w_o_rnn, w_out, w_up, w_down, g_final):
    n_lat = x.shape[1]
    rows = n_lat // GRID_W
    row_idx = jnp.repeat(jnp.arange(rows), GRID_W)
    col_idx = jnp.tile(jnp.arange(GRID_W), rows)
    cos, sin = rope_tables(row_idx, col_idx)

    for layer in range(DEPTH):
        update_ctx = layer < DEPTH - 1
        mod_lat = (jax.nn.silu(c) @ w_mod[layer] + b_mod[layer])[:, None, :]
        mod_ctx = jax.nn.silu(c_ctx) @ w_mod[layer] + b_mod[layer]
        sh_a, sc_a, ga_a, sh_f, sc_f, ga_f = jnp.split(mod_lat, N_MOD, axis=-1)
        csh_a, csc_a, cga_a, csh_f, csc_f, cga_f = jnp.split(mod_ctx, N_MOD, axis=-1)

        h_lat = modulate(rms_norm(x, g_mix[layer]), sh_a, sc_a)
        h_ctx = modulate(rms_norm(ctx, g_mix[layer]), csh_a, csc_a)
        mix_lat, mix_ctx = mixer(h_lat, h_ctx, cos, sin, w_in[layer], q_gain[layer], k_gain[layer],
                                 conv_w[layer], conv_b[layer], w_rg[layer], b_rg[layer],
                                 w_ig[layer], b_ig[layer], lru_lambda[layer], w_o_attn[layer],
                                 w_o_rnn[layer], w_out[layer], update_ctx)
        x = x + ga_a * mix_lat
        x = x + ga_f * sq_relu_mlp(modulate(rms_norm(x, g_mlp[layer]), sh_f, sc_f),
                                   w_up[layer], w_down[layer])
        if update_ctx:
            ctx = ctx + cga_a * mix_ctx
            ctx = ctx + cga_f * sq_relu_mlp(modulate(rms_norm(ctx, g_mlp[layer]), csh_f, csc_f),
                                            w_up[layer], w_down[layer])
    return rms_norm(x, g_final)
```

```python
import functools

import jax
import jax.numpy as jnp
from jax import lax
from jax.experimental import pallas as pl
from jax.experimental.pallas import tpu as pltpu

F32 = jnp.float32
BF16 = jnp.bfloat16

D_MODEL = 2048
GRID_W = 64
HEAD_DIM = 128
N_Q_HEADS = 16
N_KV_HEADS = 4
Q_GROUP = N_Q_HEADS // N_KV_HEADS
ATTN_WIDTH = N_Q_HEADS * HEAD_DIM
KV_WIDTH = N_KV_HEADS * HEAD_DIM
D_RNN = D_MODEL
N_RNN_BLOCKS = 16
RNN_BLOCK = D_RNN // N_RNN_BLOCKS
CONV_WIDTH = 4
LRU_C = 8.0
D_FF = 4 * D_MODEL
ROPE_THETA = 10000.0
NORM_EPS = 1e-6
N_MOD = 6
MOD_ROWS = 16

COL_Q = 0
COL_K = ATTN_WIDTH
COL_V = ATTN_WIDTH + KV_WIDTH
COL_XR = ATTN_WIDTH + 2 * KV_WIDTH
COL_XG = COL_XR + D_RNN
COL_GL = COL_XG + D_RNN

VMEM_LIMIT_BYTES = 56 * 1024 * 1024


def _cparams(*sem):
    return pltpu.CompilerParams(dimension_semantics=sem, vmem_limit_bytes=VMEM_LIMIT_BYTES)


def _rms(x, eps=NORM_EPS):
    return x * lax.rsqrt(jnp.mean(x * x, axis=-1, keepdims=True) + eps)


def _mod_kernel(c_ref, w_ref, b_ref, o_ref):
    c = c_ref[...]
    act = c * jax.nn.sigmoid(c)
    o_ref[...] = (
        jnp.dot(act.astype(BF16), w_ref[...].astype(BF16), preferred_element_type=F32) + b_ref[...]
    )


def _modulation(c_rows, w_mod, b_mod):
    n = w_mod.shape[1]
    tn = 1024
    return pl.pallas_call(
        _mod_kernel,
        grid=(n // tn,),
        in_specs=[
            pl.BlockSpec((MOD_ROWS, D_MODEL), lambda j: (0, 0)),
            pl.BlockSpec((D_MODEL, tn), lambda j: (0, j)),
            pl.BlockSpec((1, tn), lambda j: (0, j)),
        ],
        out_specs=pl.BlockSpec((MOD_ROWS, tn), lambda j: (0, j)),
        out_shape=jax.ShapeDtypeStruct((MOD_ROWS, n), F32),
        compiler_params=_cparams("arbitrary"),
        name="modulation",
    )(c_rows, w_mod, b_mod)


def _norm_mod_kernel(x_ref, g_ref, sh_ref, sc_ref, o_ref):
    y = _rms(x_ref[...]) * g_ref[...]
    o_ref[...] = (y * (1.0 + sc_ref[...]) + sh_ref[...]).astype(o_ref.dtype)


def _norm_mod(x2d, g, mod3, row_of_tile, tm):
    m = x2d.shape[0]
    return pl.pallas_call(
        _norm_mod_kernel,
        grid=(m // tm,),
        in_specs=[
            pl.BlockSpec((tm, D_MODEL), lambda i: (i, 0)),
            pl.BlockSpec((1, D_MODEL), lambda i: (0, 0)),
            pl.BlockSpec((None, 1, D_MODEL), lambda i: (row_of_tile(i), 0, 0)),
            pl.BlockSpec((None, 1, D_MODEL), lambda i: (row_of_tile(i), 0, 1)),
        ],
        out_specs=pl.BlockSpec((tm, D_MODEL), lambda i: (i, 0)),
        out_shape=jax.ShapeDtypeStruct((m, D_MODEL), BF16),
        compiler_params=_cparams("arbitrary"),
        name="norm_mod",
    )(x2d, g, mod3, mod3)


def _rope(x, cos, sin_signed):
    w = x.shape[-1]
    lane = lax.broadcasted_iota(jnp.int32, x.shape, 1)
    swapped = jnp.where((lane & 1) == 0, pltpu.roll(x, w - 1, 1), pltpu.roll(x, 1, 1))
    return x * cos + swapped * sin_signed


def _proj_heads_kernel(h_ref, w_ref, gain_ref, cos_ref, sin_ref, o_ref, *, rope, out_scale):
    acc = jnp.dot(h_ref[...], w_ref[...], preferred_element_type=F32)
    for hd in range(acc.shape[1] // HEAD_DIM):
        sl = slice(hd * HEAD_DIM, (hd + 1) * HEAD_DIM)
        y = _rms(acc[:, sl]) * gain_ref[...]
        if rope:
            y = _rope(y, cos_ref[...], sin_ref[...])
        if out_scale != 1.0:
            y = y * out_scale
        o_ref[:, sl] = y.astype(o_ref.dtype)


def _proj_plain_kernel(h_ref, w_ref, o_ref, *, act):
    acc = jnp.dot(h_ref[...], w_ref[...], preferred_element_type=F32)
    if act == "gelu":
        acc = jax.nn.gelu(acc)
    elif act == "sigmoid":
        acc = jax.nn.sigmoid(acc)
    o_ref[...] = acc.astype(o_ref.dtype)


def _proj(h, w_in, col0, ncols, tm, tn, out_dtype, act=None):
    m = h.shape[0]
    c0 = col0 // tn
    return pl.pallas_call(
        functools.partial(_proj_plain_kernel, act=act),
        grid=(m // tm, ncols // tn),
        in_specs=[
            pl.BlockSpec((tm, D_MODEL), lambda i, j: (i, 0)),
            pl.BlockSpec((D_MODEL, tn), lambda i, j: (0, c0 + j)),
        ],
        out_specs=pl.BlockSpec((tm, tn), lambda i, j: (i, j)),
        out_shape=jax.ShapeDtypeStruct((m, ncols), out_dtype),
        compiler_params=_cparams("arbitrary", "arbitrary"),
        name="proj_" + (act or "plain"),
    )(h, w_in)


def _proj_heads(h, w_in, col0, ncols, tm, tn, gain, cos, sin_signed, rope, out_scale, seq):
    m = h.shape[0]
    c0 = col0 // tn
    tiles_per_seq = seq // tm
    return pl.pallas_call(
        functools.partial(_proj_heads_kernel, rope=rope, out_scale=out_scale),
        grid=(m // tm, ncols // tn),
        in_specs=[
            pl.BlockSpec((tm, D_MODEL), lambda i, j: (i, 0)),
            pl.BlockSpec((D_MODEL, tn), lambda i, j: (0, c0 + j)),
            pl.BlockSpec((1, HEAD_DIM), lambda i, j: (0, 0)),
            pl.BlockSpec((tm, HEAD_DIM), lambda i, j: (i % tiles_per_seq, 0)),
            pl.BlockSpec((tm, HEAD_DIM), lambda i, j: (i % tiles_per_seq, 0)),
        ],
        out_specs=pl.BlockSpec((tm, tn), lambda i, j: (i, j)),
        out_shape=jax.ShapeDtypeStruct((m, ncols), BF16),
        compiler_params=_cparams("arbitrary", "arbitrary"),
        name="proj_heads_rope" if rope else "proj_heads",
    )(h, w_in, gain, cos, sin_signed)


def _attn_kernel(q_ref, kc_ref, kl_ref, vc_ref, vl_ref, o_ref):
    kc = kc_ref[...]
    kl = kl_ref[...]
    vc = vc_ref[...]
    vl = vl_ref[...]
    dn = (((1,), (1,)), ((), ()))
    for g in range(Q_GROUP):
        sl = slice(g * HEAD_DIM, (g + 1) * HEAD_DIM)
        q = q_ref[:, sl]
        s_c = lax.dot_general(q, kc, dn, preferred_element_type=F32)
        s_l = lax.dot_general(q, kl, dn, preferred_element_type=F32)
        m = jnp.maximum(jnp.max(s_c, axis=-1, keepdims=True), jnp.max(s_l, axis=-1, keepdims=True))
        p_c = jnp.exp(s_c - m)
        p_l = jnp.exp(s_l - m)
        denom = jnp.sum(p_c, axis=-1, keepdims=True) + jnp.sum(p_l, axis=-1, keepdims=True)
        o = jnp.dot(p_c.astype(BF16), vc, preferred_element_type=F32)
        o = o + jnp.dot(p_l.astype(BF16), vl, preferred_element_type=F32)
        o_ref[:, sl] = (o / denom).astype(o_ref.dtype)


def _attention(q, k_ctx, k_lat, v_ctx, v_lat, tq):
    b, s, _ = q.shape
    l = k_ctx.shape[1]
    gw = Q_GROUP * HEAD_DIM
    return pl.pallas_call(
        _attn_kernel,
        grid=(b, N_KV_HEADS, s // tq),
        in_specs=[
            pl.BlockSpec((None, tq, gw), lambda bi, h, qi: (bi, qi, h)),
            pl.BlockSpec((None, l, HEAD_DIM), lambda bi, h, qi: (bi, 0, h)),
            pl.BlockSpec((None, s, HEAD_DIM), lambda bi, h, qi: (bi, 0, h)),
            pl.BlockSpec((None, l, HEAD_DIM), lambda bi, h, qi: (bi, 0, h)),
            pl.BlockSpec((None, s, HEAD_DIM), lambda bi, h, qi: (bi, 0, h)),
        ],
        out_specs=pl.BlockSpec((None, tq, gw), lambda bi, h, qi: (bi, qi, h)),
        out_shape=jax.ShapeDtypeStruct((b, s, ATTN_WIDTH), BF16),
        compiler_params=_cparams("arbitrary", "arbitrary", "arbitrary"),
        name="attention",
    )(q, k_ctx, k_lat, v_ctx, v_lat)


def _conv4(x, w_ref, b_ref):
    t = x.shape[0]
    row = lax.broadcasted_iota(jnp.int32, x.shape, 0)
    xm1 = jnp.where(row >= 1, pltpu.roll(x, 1, 0), 0.0)
    xp1 = jnp.where(row < t - 1, pltpu.roll(x, t - 1, 0), 0.0)
    xp2 = jnp.where(row < t - 2, pltpu.roll(x, t - 2, 0), 0.0)
    return b_ref[...] + xm1 * w_ref[0:1, :] + x * w_ref[1:2, :] + xp1 * w_ref[2:3, :] + xp2 * w_ref[3:4, :]


def _softplus(y):
    return jnp.maximum(y, 0.0) + jnp.log1p(jnp.exp(-jnp.abs(y)))


def _block_scan(a, b, carry, reverse):
    row = lax.broadcasted_iota(jnp.int32, a.shape, 0)
    for k in (1, 2, 4):
        if reverse:
            keep = row < 8 - k
            a_sh = pltpu.roll(a, 8 - k, 0)
            b_sh = pltpu.roll(b, 8 - k, 0)
        else:
            keep = row >= k
            a_sh = pltpu.roll(a, k, 0)
            b_sh = pltpu.roll(b, k, 0)
        b = b + a * jnp.where(keep, b_sh, 0.0)
        a = a * jnp.where(keep, a_sh, 1.0)
    h = a * carry + b
    edge = h[0:1, :] if reverse else h[7:8, :]
    return h, jnp.broadcast_to(edge, h.shape)


def _rnn_kernel(xl_ref, xc_ref, cw_ref, cb_ref, wg_ref, bg_ref, lam_ref, xg_ref, o_ref,
                af_ref, bf_ref, ab_ref, bb_ref, hf_ref):
    n_ctx = xc_ref.shape[0]
    n_lat = xl_ref.shape[0]
    c = RNN_BLOCK

    def coeffs(x_ref, row0):
        xc = _conv4(x_ref[...], cw_ref, cb_ref)
        z = jnp.dot(xc.astype(BF16), wg_ref[...], preferred_element_type=F32) + bg_ref[...]
        for d, (a_ref, b_ref) in enumerate(((af_ref, bf_ref), (ab_ref, bb_ref))):
            r = jax.nn.sigmoid(z[:, (2 * d) * c:(2 * d + 1) * c])
            i = jax.nn.sigmoid(z[:, (2 * d + 1) * c:(2 * d + 2) * c])
            log_a = (-LRU_C) * r * _softplus(-lam_ref[:, d * c:(d + 1) * c])
            a_ref[pl.ds(row0, x_ref.shape[0]), :] = jnp.exp(log_a)
            th = jnp.tanh(log_a)
            b_ref[pl.ds(row0, x_ref.shape[0]), :] = jnp.sqrt(-2.0 * th / (1.0 - th)) * (i * xc)

    coeffs(xc_ref, 0)
    coeffs(xl_ref, n_ctx)

    zero = jnp.zeros((8, c), F32)

    def fwd_ctx(j, carry):
        r0 = pl.multiple_of(j * 8, 8)
        _, carry = _block_scan(af_ref[pl.ds(r0, 8), :], bf_ref[pl.ds(r0, 8), :], carry, False)
        return carry

    carry_f = lax.fori_loop(0, n_ctx // 8, fwd_ctx, zero)

    def fwd_lat(j, carry):
        r0 = pl.multiple_of(n_ctx + j * 8, 8)
        h, carry = _block_scan(af_ref[pl.ds(r0, 8), :], bf_ref[pl.ds(r0, 8), :], carry, False)
        hf_ref[pl.ds(pl.multiple_of(j * 8, 8), 8), :] = h
        return carry

    lax.fori_loop(0, n_lat // 8, fwd_lat, carry_f)

    def bwd_ctx(j, carry):
        r0 = pl.multiple_of(n_ctx - 8 - j * 8, 8)
        _, carry = _block_scan(ab_ref[pl.ds(r0, 8), :], bb_ref[pl.ds(r0, 8), :], carry, True)
        return carry

    carry_b = lax.fori_loop(0, n_ctx // 8, bwd_ctx, zero)

    def bwd_lat(j, carry):
        l0 = pl.multiple_of(n_lat - 8 - j * 8, 8)
        r0 = pl.multiple_of(n_ctx + l0, 8)
        h, carry = _block_scan(ab_ref[pl.ds(r0, 8), :], bb_ref[pl.ds(r0, 8), :], carry, True)
        hf_ref[pl.ds(l0, 8), :] = hf_ref[pl.ds(l0, 8), :] + h
        return carry

    lax.fori_loop(0, n_lat // 8, bwd_lat, carry_b)

    o_ref[...] = (hf_ref[...] * xg_ref[...].astype(F32)).astype(o_ref.dtype)


def _rnn_branch(xr_lat, xr_ctx, conv_w, conv_b, wg, bg, lam, xg_act):
    b, s, _ = xr_lat.shape
    l = xr_ctx.shape[1]
    c = RNN_BLOCK
    return pl.pallas_call(
        _rnn_kernel,
        grid=(b, N_RNN_BLOCKS),
        in_specs=[
            pl.BlockSpec((None, s, c), lambda bi, ci: (bi, 0, ci)),
            pl.BlockSpec((None, l, c), lambda bi, ci: (bi, 0, ci)),
            pl.BlockSpec((CONV_WIDTH, c), lambda bi, ci: (0, ci)),
            pl.BlockSpec((1, c), lambda bi, ci: (0, ci)),
            pl.BlockSpec((None, c, 4 * c), lambda bi, ci: (ci, 0, 0)),
            pl.BlockSpec((None, 1, 4 * c), lambda bi, ci: (ci, 0, 0)),
            pl.BlockSpec((None, 1, 2 * c), lambda bi, ci: (ci, 0, 0)),
            pl.BlockSpec((None, s, c), lambda bi, ci: (bi, 0, ci)),
        ],
        out_specs=pl.BlockSpec((None, s, c), lambda bi, ci: (bi, 0, ci)),
        out_shape=jax.ShapeDtypeStruct((b, s, D_RNN), BF16),
        scratch_shapes=[pltpu.VMEM((s + l, c), F32)] * 4 + [pltpu.VMEM((s, c), F32)],
        compiler_params=_cparams("arbitrary", "arbitrary"),
        name="rnn_branch",
    )(xr_lat, xr_ctx, conv_w, conv_b, wg, bg, lam, xg_act)


def _merge_kernel(ao_ref, u_ref, wa_ref, wr_ref, ga_ref, gr_ref, o_ref):
    ya = jnp.dot(ao_ref[...], wa_ref[...], preferred_element_type=F32)
    yr = jnp.dot(u_ref[...], wr_ref[...], preferred_element_type=F32)
    o_ref[...] = (ga_ref[...].astype(F32) * ya + gr_ref[...].astype(F32) * yr).astype(o_ref.dtype)


def _merge(ao, u, w_o_attn, w_o_rnn, gates, tm, tn):
    m = ao.shape[0]
    nb = D_MODEL // tn
    return pl.pallas_call(
        _merge_kernel,
        grid=(m // tm, nb),
        in_specs=[
            pl.BlockSpec((tm, ATTN_WIDTH), lambda i, j: (i, 0)),
            pl.BlockSpec((tm, D_RNN), lambda i, j: (i, 0)),
            pl.BlockSpec((ATTN_WIDTH, tn), lambda i, j: (0, j)),
            pl.BlockSpec((D_RNN, tn), lambda i, j: (0, j)),
            pl.BlockSpec((tm, tn), lambda i, j: (i, j)),
            pl.BlockSpec((tm, tn), lambda i, j: (i, nb + j)),
        ],
        out_specs=pl.BlockSpec((tm, tn), lambda i, j: (i, j)),
        out_shape=jax.ShapeDtypeStruct((m, D_MODEL), BF16),
        compiler_params=_cparams("arbitrary", "arbitrary"),
        name="merge",
    )(ao, u, w_o_attn, w_o_rnn, gates, gates)


def _outproj_kernel(m_ref, w_ref, x_ref, ga_ref, g_ref, sh_ref, sc_ref, x1_ref, h2_ref):
    y = jnp.dot(m_ref[...], w_ref[...], preferred_element_type=F32)
    x1 = x_ref[...] + ga_ref[...] * y
    x1_ref[...] = x1
    h2 = _rms(x1) * g_ref[...]
    h2_ref[...] = (h2 * (1.0 + sc_ref[...]) + sh_ref[...]).astype(h2_ref.dtype)


def _outproj(mg, w_out, x2d, mod3, g_mlp, tm, seq):
    m = mg.shape[0]
    tiles_per_seq = seq // tm
    row = lambda i: i // tiles_per_seq
    return pl.pallas_call(
        _outproj_kernel,
        grid=(m // tm,),
        in_specs=[
            pl.BlockSpec((tm, D_MODEL), lambda i: (i, 0)),
            pl.BlockSpec((D_MODEL, D_MODEL), lambda i: (0, 0)),
            pl.BlockSpec((tm, D_MODEL), lambda i: (i, 0)),
            pl.BlockSpec((None, 1, D_MODEL), lambda i: (row(i), 0, 2)),
            pl.BlockSpec((1, D_MODEL), lambda i: (0, 0)),
            pl.BlockSpec((None, 1, D_MODEL), lambda i: (row(i), 0, 3)),
            pl.BlockSpec((None, 1, D_MODEL), lambda i: (row(i), 0, 4)),
        ],
        out_specs=[
            pl.BlockSpec((tm, D_MODEL), lambda i: (i, 0)),
            pl.BlockSpec((tm, D_MODEL), lambda i: (i, 0)),
        ],
        out_shape=[
            jax.ShapeDtypeStruct((m, D_MODEL), F32),
            jax.ShapeDtypeStruct((m, D_MODEL), BF16),
        ],
        compiler_params=_cparams("arbitrary"),
        name="outproj",
    )(mg, w_out, x2d, mod3, g_mlp, mod3, mod3)


def _mlp_kernel(h_ref, wu_ref, wd_ref, x1_ref, ga_ref, gf_ref, o_ref):
    f = pl.program_id(1)
    a = jnp.dot(h_ref[...], wu_ref[...], preferred_element_type=F32)
    a = jnp.square(jnp.maximum(a, 0.0)).astype(BF16)
    y = jnp.dot(a, wd_ref[...], preferred_element_type=F32)

    @pl.when(f == 0)
    def _():
        o_ref[...] = y

    @pl.when(f > 0)
    def _():
        o_ref[...] += y

    @pl.when(f == pl.num_programs(1) - 1)
    def _():
        x2 = x1_ref[...] + ga_ref[...] * o_ref[...]
        o_ref[...] = _rms(x2) * gf_ref[...]


def _mlp(h2, w_up, w_down, x1, mod3, g_final, tm, tf, seq):
    m = h2.shape[0]
    tiles_per_seq = seq // tm
    return pl.pallas_call(
        _mlp_kernel,
        grid=(m // tm, D_FF // tf),
        in_specs=[
            pl.BlockSpec((tm, D_MODEL), lambda i, f: (i, 0)),
            pl.BlockSpec((D_MODEL, tf), lambda i, f: (0, f)),
            pl.BlockSpec((tf, D_MODEL), lambda i, f: (f, 0)),
            pl.BlockSpec((tm, D_MODEL), lambda i, f: (i, 0)),
            pl.BlockSpec((None, 1, D_MODEL), lambda i, f: (i // tiles_per_seq, 0, 5)),
            pl.BlockSpec((1, D_MODEL), lambda i, f: (0, 0)),
        ],
        out_specs=pl.BlockSpec((tm, D_MODEL), lambda i, f: (i, 0)),
        out_shape=jax.ShapeDtypeStruct((m, D_MODEL), F32),
        compiler_params=_cparams("arbitrary", "arbitrary"),
        name="mlp",
    )(h2, w_up, w_down, x1, mod3, g_final)


def _rope_tables(n_lat):
    n_freq = HEAD_DIM // 4
    inv_freq = ROPE_THETA ** (-jnp.arange(n_freq, dtype=F32) / n_freq)
    pos = jnp.arange(n_lat)
    ang = jnp.concatenate(
        [(pos // GRID_W).astype(F32)[:, None] * inv_freq, (pos % GRID_W).astype(F32)[:, None] * inv_freq],
        axis=-1,
    )
    cos = jnp.repeat(jnp.cos(ang), 2, axis=-1)
    sin = jnp.repeat(jnp.sin(ang), 2, axis=-1)
    sign = jnp.tile(jnp.array([-1.0, 1.0], F32), HEAD_DIM // 2)
    return cos, sin * sign


def kernel(x, c, ctx, c_ctx, w_mod, b_mod, g_mix, g_mlp, w_in, q_gain, k_gain, conv_w, conv_b,
           w_rg, b_rg, w_ig, b_ig, lru_lambda, w_o_attn, w_o_rnn, w_out, w_up, w_down, g_final):
    assert w_mod.shape[0] == 1, "single-layer block"
    b, s, d = x.shape
    l = ctx.shape[1]
    x2d = x.reshape(b * s, d)
    ctx2d = ctx.reshape(b * l, d)

    w_in_b = w_in[0].astype(BF16)
    w_oa = w_o_attn[0].astype(BF16)
    w_or = w_o_rnn[0].astype(BF16)
    w_ou = w_out[0].astype(BF16)
    w_u = w_up[0].astype(BF16)
    w_d = w_down[0].astype(BF16)
    wg = jnp.concatenate([w_rg[0, 0], w_ig[0, 0], w_rg[0, 1], w_ig[0, 1]], axis=-1).astype(BF16)
    nb, rb = N_RNN_BLOCKS, RNN_BLOCK
    bg = jnp.concatenate(
        [b_rg[0, 0].reshape(nb, 1, rb), b_ig[0, 0].reshape(nb, 1, rb),
         b_rg[0, 1].reshape(nb, 1, rb), b_ig[0, 1].reshape(nb, 1, rb)], axis=-1)
    lam = jnp.concatenate(
        [lru_lambda[0, 0].reshape(nb, 1, rb), lru_lambda[0, 1].reshape(nb, 1, rb)], axis=-1)
    cos, sin_signed = _rope_tables(s)

    c_rows = jnp.concatenate([c, c_ctx[None, :], jnp.zeros((MOD_ROWS - b - 1, d), F32)], axis=0)
    mod = _modulation(c_rows, w_mod[0], b_mod)
    mod3 = mod.reshape(MOD_ROWS, 1, N_MOD * d)

    tm = 1024
    h_lat = _norm_mod(x2d, g_mix, mod3, lambda i: i // (s // tm), tm)
    h_ctx = _norm_mod(ctx2d, g_mix, mod3, lambda i: b, tm)

    scale = HEAD_DIM ** -0.5
    q = _proj_heads(h_lat, w_in_b, COL_Q, ATTN_WIDTH, tm, 1024, q_gain, cos, sin_signed, True, scale, s)
    k_lat = _proj_heads(h_lat, w_in_b, COL_K, KV_WIDTH, tm, 512, k_gain, cos, sin_signed, True, 1.0, s)
    k_ctx = _proj_heads(h_ctx, w_in_b, COL_K, KV_WIDTH, tm, 512, k_gain, cos, sin_signed, False, 1.0, s)
    v_lat = _proj(h_lat, w_in_b, COL_V, KV_WIDTH, tm, 512, BF16)
    v_ctx = _proj(h_ctx, w_in_b, COL_V, KV_WIDTH, tm, 512, BF16)
    xr_lat = _proj(h_lat, w_in_b, COL_XR, D_RNN, tm, 1024, F32)
    xr_ctx = _proj(h_ctx, w_in_b, COL_XR, D_RNN, tm, 1024, F32)
    xg_act = _proj(h_lat, w_in_b, COL_XG, D_RNN, tm, 1024, BF16, act="gelu")
    gates = _proj(h_lat, w_in_b, COL_GL, 2 * D_MODEL, tm, 1024, BF16, act="sigmoid")

    ao = _attention(
        q.reshape(b, s, ATTN_WIDTH), k_ctx.reshape(b, l, KV_WIDTH), k_lat.reshape(b, s, KV_WIDTH),
        v_ctx.reshape(b, l, KV_WIDTH), v_lat.reshape(b, s, KV_WIDTH), tq=256)

    u = _rnn_branch(xr_lat.reshape(b, s, D_RNN), xr_ctx.reshape(b, l, D_RNN), conv_w[0], conv_b,
                    wg, bg, lam, xg_act.reshape(b, s, D_RNN))

    mg = _merge(ao.reshape(b * s, ATTN_WIDTH), u.reshape(b * s, D_RNN), w_oa, w_or, gates, 512, 1024)
    x1, h2 = _outproj(mg, w_ou, x2d, mod3, g_mlp, 512, s)
    out = _mlp(h2, w_u, w_d, x1, mod3, g_final[None, :], 512, 1024, s)
    return out.reshape(b, s, d)
```
